```python
import jax, jax.numpy as jnp
from jax import lax
import numpy as np

D_MODEL = 1024
BATCH = 8
SEQ = 2048
DEPTH = 4

HEAD_DIM = 64
MOBA_HEADS = 8
SWA_HEADS = 8
SWA_KV_HEADS = 2
MOBA_WIDTH = MOBA_HEADS * HEAD_DIM
SWA_WIDTH = SWA_HEADS * HEAD_DIM
SWA_KV_WIDTH = SWA_KV_HEADS * HEAD_DIM
MIX_WIDTH = MOBA_WIDTH + SWA_WIDTH
IN_WIDTH = 3 * MOBA_WIDTH + SWA_WIDTH + 2 * SWA_KV_WIDTH
MOBA_BLOCK = 256
MOBA_TOPK = 3
MOBA_Q_CHUNK = 32
SWA_WINDOW = 128
D_FF = -(-8 * D_MODEL // (3 * 256)) * 256
ROPE_THETA = 10000.0
EPS = 1e-6
NEG_INF = -1e30
SPLIT_OFFSETS = [MOBA_WIDTH, 2 * MOBA_WIDTH, 3 * MOBA_WIDTH,
                 3 * MOBA_WIDTH + SWA_WIDTH, 3 * MOBA_WIDTH + SWA_WIDTH + SWA_KV_WIDTH]

kernel_name = "hymba_moba_swa_sink_sandwich_trunk"


def rms_norm(x, g):
    xf = x.astype(jnp.float32)
    y = xf * lax.rsqrt(jnp.mean(xf * xf, axis=-1, keepdims=True) + EPS)
    return (y * g.astype(jnp.float32)).astype(x.dtype)


def rope_tables(seq):
    inv = 1.0 / (ROPE_THETA ** (jnp.arange(0, HEAD_DIM, 2, dtype=jnp.float32) / HEAD_DIM))
    ang = jnp.arange(seq, dtype=jnp.float32)[:, None] * inv[None, :]
    return jnp.cos(ang), jnp.sin(ang)


def apply_rope(x, cos, sin):
    x1, x2 = jnp.split(x.astype(jnp.float32), 2, axis=-1)
    c = cos[None, :, None, :]
    s = sin[None, :, None, :]
    return jnp.concatenate([x1 * c - x2 * s, x2 * c + x1 * s], axis=-1).astype(x.dtype)


def moba_attention(q, k, v):
    b, s, h, d = q.shape
    nb = -(-s // MOBA_BLOCK)
    pad = nb * MOBA_BLOCK - s
    scale = d ** -0.5
    qh = jnp.transpose(q, (0, 2, 1, 3))

    def to_blocks(a):
        a = jnp.pad(jnp.transpose(a, (0, 2, 1, 3)), ((0, 0), (0, 0), (0, pad), (0, 0)))
        return a.reshape(b, h, nb, MOBA_BLOCK, d)

    kb, vb = to_blocks(k), to_blocks(v)
    k_mean = jnp.mean(kb.astype(jnp.float32), axis=3)
    q_blk = jnp.arange(s) // MOBA_BLOCK
    gate = jnp.einsum('bhtd,bhnd->bhtn', qh.astype(jnp.float32), k_mean)
    fully_past = jnp.arange(nb)[None, :] < q_blk[:, None]
    gate = jnp.where(fully_past, gate, -jnp.inf)
    n_sel = min(MOBA_TOPK, nb)
    _, sel = lax.top_k(gate, n_sel)

    nc = s // MOBA_Q_CHUNK

    def to_chunks(a):
        a = a.reshape(a.shape[:2] + (nc, MOBA_Q_CHUNK) + a.shape[3:])
        return jnp.moveaxis(a, 2, 0)

    bi = jnp.arange(b)[:, None, None, None]
    hi = jnp.arange(h)[None, :, None, None]

    def chunk_fn(args):
        qc, selc, ci = args
        t = ci * MOBA_Q_CHUNK + jnp.arange(MOBA_Q_CHUNK)
        blk = t[0] // MOBA_BLOCK
        ks = kb[bi, hi, selc]
        vs = vb[bi, hi, selc]
        valid = jnp.arange(n_sel) < blk
        lp = jnp.einsum('bhcd,bhcjkd->bhcjk', qc, ks,
                        preferred_element_type=jnp.float32) * scale
        lp = jnp.where(valid[None, None, None, :, None], lp, NEG_INF)
        lp = lp.reshape(b, h, MOBA_Q_CHUNK, n_sel * MOBA_BLOCK)
        k_own = lax.dynamic_index_in_dim(kb, blk, axis=2, keepdims=False)
        v_own = lax.dynamic_index_in_dim(vb, blk, axis=2, keepdims=False)
        lo = jnp.einsum('bhcd,bhkd->bhck', qc, k_own,
                        preferred_element_type=jnp.float32) * scale
        kpos = blk * MOBA_BLOCK + jnp.arange(MOBA_BLOCK)
        lo = jnp.where(kpos[None, :] <= t[:, None], lo, NEG_INF)
        p = jax.nn.softmax(jnp.concatenate([lp, lo], axis=-1), axis=-1)
        pp = p[..., :n_sel * MOBA_BLOCK].reshape(b, h, MOBA_Q_CHUNK, n_sel, MOBA_BLOCK)
        po = p[..., n_sel * MOBA_BLOCK:]
        return (jnp.einsum('bhcjk,bhcjkd->bhcd', pp.astype(v.dtype), vs)
                + jnp.einsum('bhck,bhkd->bhcd', po.astype(v.dtype), v_own))

    o = lax.map(chunk_fn, (to_chunks(qh), to_chunks(sel), jnp.arange(nc)))
    o = jnp.moveaxis(o, 0, 2).reshape(b, h, s, d)
    return jnp.transpose(o, (0, 2, 1, 3))


def swa_sink_attention(q, k, v, sinks):
    b, s, hq, d = q.shape
    hkv = k.shape[2]
    g = hq // hkv
    w = SWA_WINDOW
    nblk = s // w
    scale = d ** -0.5
    qb = q.reshape(b, nblk, w, hkv, g, d)

    def band(a):
        ab = a.reshape(b, nblk, w, hkv, d)
        prev = jnp.pad(ab, ((0, 0), (1, 0), (0, 0), (0, 0), (0, 0)))[:, :-1]
        return jnp.concatenate([prev, ab], axis=2)

    kb, vb = band(k), band(v)
    logits = jnp.einsum('bnqhgd,bnphd->bhgnqp', qb, kb,
                        preferred_element_type=jnp.float32) * scale
    qpos = jnp.arange(nblk)[:, None, None] * w + jnp.arange(w)[None, :, None]
    kpos = jnp.arange(nblk)[:, None, None] * w - w + jnp.arange(2 * w)[None, None, :]
    diff = qpos - kpos
    mask = (diff >= 0) & (diff < w) & (kpos >= 0)
    logits = jnp.where(mask, logits, NEG_INF)
    sink = jnp.broadcast_to(sinks.astype(jnp.float32).reshape(hkv, g)[None, :, :, None, None, None],
                            logits.shape[:-1] + (1,))
    p = jax.nn.softmax(jnp.concatenate([logits, sink], axis=-1), axis=-1)[..., :-1]
    o = jnp.einsum('bhgnqp,bnphd->bnqhgd', p.astype(v.dtype), vb)
    return o.reshape(b, s, hq, d)


def hybrid_layer(x, cos, sin, w_in, b_in, w_out, g_attn_pre, g_attn_post, g_moba_out,
                 g_swa_out, sinks, g_ffn_pre, g_ffn_post, w_gate, w_up, w_down):
    b, s, _ = x.shape
    h = rms_norm(x, g_attn_pre)
    proj = jnp.einsum('bsd,de->bse', h, w_in) + b_in
    mq, mk, mv, sq, sk, sv = jnp.split(proj, SPLIT_OFFSETS, axis=-1)
    mq = apply_rope(mq.reshape(b, s, MOBA_HEADS, HEAD_DIM), cos, sin)
    mk = apply_rope(mk.reshape(b, s, MOBA_HEADS, HEAD_DIM), cos, sin)
    mv = mv.reshape(b, s, MOBA_HEADS, HEAD_DIM)
    sq = apply_rope(sq.reshape(b, s, SWA_HEADS, HEAD_DIM), cos, sin)
    sk = apply_rope(sk.reshape(b, s, SWA_KV_HEADS, HEAD_DIM), cos, sin)
    sv = sv.reshape(b, s, SWA_KV_HEADS, HEAD_DIM)
    mo = rms_norm(moba_attention(mq, mk, mv).reshape(b, s, MOBA_WIDTH), g_moba_out)
    so = rms_norm(swa_sink_attention(sq, sk, sv, sinks).reshape(b, s, SWA_WIDTH), g_swa_out)
    mix = jnp.concatenate([mo, so], axis=-1)
    x = x + rms_norm(jnp.einsum('bse,ed->bsd', mix, w_out), g_attn_post)
    h = rms_norm(x, g_ffn_pre)
    f = jax.nn.silu(jnp.einsum('bsd,df->bsf', h, w_gate)) * jnp.einsum('bsd,df->bsf', h, w_up)
    x = x + rms_norm(jnp.einsum('bsf,fd->bsd', f, w_down), g_ffn_post)
    return x


def setup_inputs(seed: int = 0) -> dict:
    key = jax.random.key(seed)
    ks = jax.random.split(key, 16)
    f32 = jnp.float32

    def nrm(k, shape, scale):
        return jax.random.normal(k, shape, f32) * scale

    def gain(k, width):
        return 1.0 + 0.05 * jax.random.normal(k, (DEPTH, width), f32)

    return {
        "x": jax.random.normal(ks[0], (BATCH, SEQ, D_MODEL), f32),
        "w_in": nrm(ks[1], (DEPTH, D_MODEL, IN_WIDTH), D_MODEL ** -0.5),
        "b_in": nrm(ks[2], (DEPTH, IN_WIDTH), 0.02),
        "w_out": nrm(ks[3], (DEPTH, MIX_WIDTH, D_MODEL), MIX_WIDTH ** -0.5),
        "g_attn_pre": gain(ks[4], D_MODEL),
        "g_attn_post": gain(ks[5], D_MODEL),
        "g_moba_out": gain(ks[6], MOBA_WIDTH),
        "g_swa_out": gain(ks[7], SWA_WIDTH),
        "attn_sinks": nrm(ks[8], (DEPTH, SWA_HEADS), 0.5),
        "g_ffn_pre": gain(ks[9], D_MODEL),
        "g_ffn_post": gain(ks[10], D_MODEL),
        "w_gate": nrm(ks[11], (DEPTH, D_MODEL, D_FF), D_MODEL ** -0.5),
        "w_up": nrm(ks[12], (DEPTH, D_MODEL, D_FF), D_MODEL ** -0.5),
        "w_down": nrm(ks[13], (DEPTH, D_FF, D_MODEL), D_FF ** -0.5),
    }


def reference(x, w_in, b_in, w_out, g_attn_pre, g_attn_post, g_moba_out, g_swa_out,
              attn_sinks, g_ffn_pre, g_ffn_post, w_gate, w_up, w_down):
    cos, sin = rope_tables(x.shape[1])
    for i in range(DEPTH):
        x = hybrid_layer(x, cos, sin, w_in[i], b_in[i], w_out[i], g_attn_pre[i], g_attn_post[i],
                         g_moba_out[i], g_swa_out[i], attn_sinks[i], g_ffn_pre[i], g_ffn_post[i],
                         w_gate[i], w_up[i], w_down[i])
    return x
```

```python
import functools

import jax
import jax.numpy as jnp
from jax import lax
from jax.experimental import pallas as pl
from jax.experimental.pallas import tpu as pltpu

D_MODEL = 1024
DEPTH = 4
HEAD_DIM = 64
MOBA_HEADS = 8
SWA_HEADS = 8
SWA_KV_HEADS = 2
MOBA_WIDTH = MOBA_HEADS * HEAD_DIM
SWA_WIDTH = SWA_HEADS * HEAD_DIM
SWA_KV_WIDTH = SWA_KV_HEADS * HEAD_DIM
MIX_WIDTH = MOBA_WIDTH + SWA_WIDTH
IN_WIDTH = 3 * MOBA_WIDTH + SWA_WIDTH + 2 * SWA_KV_WIDTH
MOBA_BLOCK = 256
MOBA_TOPK = 3
SWA_WINDOW = 128
D_FF = 2816
ROPE_THETA = 10000.0
EPS = 1e-6
NEG_INF = -1e30

LANES = 128
ROW_TILE = 512
VMEM_LIMIT = 56 * 1024 * 1024

BF16 = jnp.bfloat16
F32 = jnp.float32

_MQ, _MK, _MV = 0, MOBA_WIDTH, 2 * MOBA_WIDTH
_SQ = 3 * MOBA_WIDTH
_SK = _SQ + SWA_WIDTH
_SV = _SK + SWA_KV_WIDTH


def _rms(x, g):
    return x * lax.rsqrt(jnp.mean(x * x, axis=-1, keepdims=True) + EPS) * g


def _inproj_body(x_ref, g_ref, w_ref, b_ref, cq_ref, sq_ref, ck_ref, sk_ref,
                 mqt_o, mk_o, mvt_o, km_o, sqt_o, sk_o, svt_o):
    h = _rms(x_ref[...], g_ref[...]).astype(BF16)
    lane = lax.broadcasted_iota(jnp.int32, (1, LANES), 1)
    first_half = (lane & (HEAD_DIM // 2)) == 0
    cq, sq, ck, sk = cq_ref[...], sq_ref[...], ck_ref[...], sk_ref[...]

    def proj(col):
        return (jnp.dot(h, w_ref[:, col:col + LANES], preferred_element_type=F32)
                + b_ref[:, col:col + LANES])

    def rope(slab, c, s):
        rot = jnp.where(first_half,
                        pltpu.roll(slab, LANES - HEAD_DIM // 2, 1),
                        pltpu.roll(slab, HEAD_DIM // 2, 1))
        return slab * c + rot * s

    n_mb = ROW_TILE // MOBA_BLOCK
    n_sb = ROW_TILE // SWA_WINDOW
    for c in range(MOBA_WIDTH // LANES):
        lo = c * LANES
        q = rope(proj(_MQ + lo), cq, sq)
        k = rope(proj(_MK + lo), ck, sk)
        v = proj(_MV + lo)
        mk_o[:, lo:lo + LANES] = k.astype(BF16)
        for r in range(n_mb):
            rows = slice(r * MOBA_BLOCK, (r + 1) * MOBA_BLOCK)
            mqt_o[r, lo:lo + LANES, :] = q[rows].T.astype(BF16)
            mvt_o[r, lo:lo + LANES, :] = v[rows].T.astype(BF16)
            km_o[r, :, lo:lo + LANES] = jnp.mean(k[rows], axis=0, keepdims=True)
    for c in range(SWA_WIDTH // LANES):
        lo = c * LANES
        q = rope(proj(_SQ + lo), cq, sq)
        for r in range(n_sb):
            rows = slice(r * SWA_WINDOW, (r + 1) * SWA_WINDOW)
            sqt_o[r, lo:lo + LANES, :] = q[rows].T.astype(BF16)
    sk_o[...] = rope(proj(_SK), ck, sk).astype(BF16)
    v = proj(_SV)
    for r in range(n_sb):
        rows = slice(r * SWA_WINDOW, (r + 1) * SWA_WINDOW)
        svt_o[r, :, :] = v[rows].T.astype(BF16)


def _inproj(x2, g, w, b, tabs, layer, seq):
    t = x2.shape[0]
    n_tiles = t // ROW_TILE
    tiles_per_seq = seq // ROW_TILE
    n_mb = ROW_TILE // MOBA_BLOCK
    n_sb = ROW_TILE // SWA_WINDOW
    tab_spec = pl.BlockSpec((ROW_TILE, LANES), lambda i: (i % tiles_per_seq, 0))
    out_shape = (
        jax.ShapeDtypeStruct((t // MOBA_BLOCK, MOBA_WIDTH, MOBA_BLOCK), BF16),
        jax.ShapeDtypeStruct((t, MOBA_WIDTH), BF16),
        jax.ShapeDtypeStruct((t // MOBA_BLOCK, MOBA_WIDTH, MOBA_BLOCK), BF16),
        jax.ShapeDtypeStruct((t // MOBA_BLOCK, 1, MOBA_WIDTH), F32),
        jax.ShapeDtypeStruct((t // SWA_WINDOW, SWA_WIDTH, SWA_WINDOW), BF16),
        jax.ShapeDtypeStruct((t, SWA_KV_WIDTH), BF16),
        jax.ShapeDtypeStruct((t // SWA_WINDOW, SWA_KV_WIDTH, SWA_WINDOW), BF16),
    )
    out_specs = (
        pl.BlockSpec((n_mb, MOBA_WIDTH, MOBA_BLOCK), lambda i: (i, 0, 0)),
        pl.BlockSpec((ROW_TILE, MOBA_WIDTH), lambda i: (i, 0)),
        pl.BlockSpec((n_mb, MOBA_WIDTH, MOBA_BLOCK), lambda i: (i, 0, 0)),
        pl.BlockSpec((n_mb, 1, MOBA_WIDTH), lambda i: (i, 0, 0)),
        pl.BlockSpec((n_sb, SWA_WIDTH, SWA_WINDOW), lambda i: (i, 0, 0)),
        pl.BlockSpec((ROW_TILE, SWA_KV_WIDTH), lambda i: (i, 0)),
        pl.BlockSpec((n_sb, SWA_KV_WIDTH, SWA_WINDOW), lambda i: (i, 0, 0)),
    )
    return pl.pallas_call(
        _inproj_body,
        grid=(n_tiles,),
        in_specs=[
            pl.BlockSpec((ROW_TILE, D_MODEL), lambda i: (i, 0)),
            pl.BlockSpec((None, 1, D_MODEL), lambda i: (layer, 0, 0)),
            pl.BlockSpec((None, D_MODEL, IN_WIDTH), lambda i: (layer, 0, 0)),
            pl.BlockSpec((None, 1, IN_WIDTH), lambda i: (layer, 0, 0)),
            tab_spec, tab_spec, tab_spec, tab_spec,
        ],
        out_specs=out_specs,
        out_shape=out_shape,
        compiler_params=pltpu.CompilerParams(
            dimension_semantics=("arbitrary",), vmem_limit_bytes=VMEM_LIMIT),
        name="inproj",
    )(x2, g, w, b, *tabs)


def _moba_body(qt_ref, k_ref, vt_ref, km_ref, o_ref, bias_ref):
    qi = pl.program_id(2)
    n_blocks = km_ref.shape[0]
    qt = qt_ref[...]
    drow = lax.broadcasted_iota(jnp.int32, (LANES, 1), 0)
    lane = lax.broadcasted_iota(jnp.int32, (1, LANES), 1)

    km = km_ref[...]
    km2 = jnp.concatenate([jnp.where(lane < HEAD_DIM, km, 0.0),
                           jnp.where(lane >= HEAD_DIM, km, 0.0)], axis=0)
    hi = km2.astype(BF16)
    r1 = km2 - hi.astype(F32)
    mid = r1.astype(BF16)
    low = (r1 - mid.astype(F32)).astype(BF16)
    g3 = jnp.dot(jnp.concatenate([hi, mid, low], axis=0), qt, preferred_element_type=F32)
    nb2 = 2 * n_blocks
    gate = g3[0:nb2] + g3[nb2:2 * nb2] + g3[2 * nb2:3 * nb2]

    n_idx = lax.broadcasted_iota(jnp.int32, (n_blocks, MOBA_BLOCK), 0)
    past = n_idx < qi
    for hh in range(2):
        gm = jnp.where(past, gate[hh * n_blocks:(hh + 1) * n_blocks], -jnp.inf)
        rank = jnp.zeros((n_blocks, MOBA_BLOCK), jnp.int32)
        for r in range(1, n_blocks):
            gr = pltpu.roll(gm, r, 0)
            nr = (n_idx - r) & (n_blocks - 1)
            beats = (gr > gm) | ((gr == gm) & (nr < n_idx))
            rank = rank + beats.astype(jnp.int32)
        sel = past & (rank < MOBA_TOPK)
        bias_ref[hh * n_blocks:(hh + 1) * n_blocks, :] = jnp.where(sel, 0.0, NEG_INF)

    qth = [jnp.where((drow >= hh * HEAD_DIM) & (drow < (hh + 1) * HEAD_DIM), qt,
                     jnp.zeros_like(qt)) for hh in range(2)]

    def scores(j, hh):
        kj = k_ref[pl.ds(pl.multiple_of(j * MOBA_BLOCK, MOBA_BLOCK), MOBA_BLOCK), :]
        return jnp.dot(kj, qth[hh], preferred_element_type=F32)

    def pv(j, hh, p):
        vt = vt_ref[j, hh * HEAD_DIM:(hh + 1) * HEAD_DIM, :]
        return jnp.dot(vt, p.astype(BF16), preferred_element_type=F32)

    kpos = lax.broadcasted_iota(jnp.int32, (MOBA_BLOCK, MOBA_BLOCK), 0)
    qpos = lax.broadcasted_iota(jnp.int32, (MOBA_BLOCK, MOBA_BLOCK), 1)
    causal = kpos <= qpos
    init = []
    for hh in range(2):
        s = jnp.where(causal, scores(qi, hh), NEG_INF)
        m = jnp.max(s, axis=0, keepdims=True)
        p = jnp.exp(s - m)
        init += [m, jnp.sum(p, axis=0, keepdims=True), pv(qi, hh, p)]

    def past_block(j, carry):
        out = []
        for hh in range(2):
            m, l, acc = carry[3 * hh:3 * hh + 3]
            s = scores(j, hh) + bias_ref[pl.ds(hh * n_blocks + j, 1), :]
            m_new = jnp.maximum(m, jnp.max(s, axis=0, keepdims=True))
            alpha = jnp.exp(m - m_new)
            p = jnp.exp(s - m_new)
            out += [m_new, alpha * l + jnp.sum(p, axis=0, keepdims=True),
                    alpha * acc + pv(j, hh, p)]
        return tuple(out)

    fin = lax.fori_loop(0, qi, past_block, tuple(init))
    ot = jnp.concatenate([fin[2] / fin[1], fin[5] / fin[4]], axis=0)
    o_ref[...] = ot.T.astype(BF16)


def _moba(qt, k, vt, km, batch, seq):
    t = k.shape[0]
    n_blocks = seq // MOBA_BLOCK
    n_pairs = MOBA_WIDTH // LANES
    return pl.pallas_call(
        _moba_body,
        grid=(batch, n_pairs, n_blocks),
        in_specs=[
            pl.BlockSpec((None, LANES, MOBA_BLOCK), lambda b, p, i: (b * n_blocks + i, p, 0)),
            pl.BlockSpec((seq, LANES), lambda b, p, i: (b, p)),
            pl.BlockSpec((n_blocks, LANES, MOBA_BLOCK), lambda b, p, i: (b, p, 0)),
            pl.BlockSpec((None, n_blocks, LANES), lambda b, p, i: (b, 0, p)),
        ],
        out_specs=pl.BlockSpec((MOBA_BLOCK, LANES), lambda b, p, i: (b * n_blocks + i, p)),
        out_shape=jax.ShapeDtypeStruct((t, MOBA_WIDTH), BF16),
        scratch_shapes=[pltpu.VMEM((2 * n_blocks, MOBA_BLOCK), F32)],
        compiler_params=pltpu.CompilerParams(
            dimension_semantics=("arbitrary", "arbitrary", "arbitrary"),
            vmem_limit_bytes=VMEM_LIMIT),
        name="moba",
    )(qt, k, vt, km)


def _swa_body(qt_ref, kc_ref, kp_ref, vtc_ref, vtp_ref, sink_ref, o_ref):
    slab = pl.program_id(1)
    w = SWA_WINDOW
    grp = SWA_HEADS // SWA_KV_HEADS
    lane = lax.broadcasted_iota(jnp.int32, (1, LANES), 1)
    kpos = lax.broadcasted_iota(jnp.int32, (2 * w, grp * w), 0)
    qpos = lax.broadcasted_iota(jnp.int32, (2 * w, grp * w), 1) & (w - 1)
    diff = kpos - qpos
    in_window = (diff >= 1) & (diff <= w)
    zeros_half = jnp.zeros((HEAD_DIM, w), BF16)

    for nb in range(ROW_TILE // w):
        if nb == 0:
            k_prev, vt_prev = kp_ref[...], vtp_ref[0]
            valid = in_window & ((kpos >= w) | (slab > 0))
        else:
            k_prev, vt_prev = kc_ref[(nb - 1) * w:nb * w, :], vtc_ref[nb - 1]
            valid = in_window
        k2 = jnp.concatenate([k_prev, kc_ref[nb * w:(nb + 1) * w, :]], axis=0)
        vt2 = jnp.concatenate([vt_prev, vtc_ref[nb]], axis=1)
        heads = []
        for g in range(SWA_KV_HEADS):
            kg = jnp.where((lane >= g * HEAD_DIM) & (lane < (g + 1) * HEAD_DIM), k2,
                           jnp.zeros_like(k2))
            cols = []
            for hh in range(grp):
                h = g * grp + hh
                qh = qt_ref[nb, h * HEAD_DIM:(h + 1) * HEAD_DIM, :]
                cols.append(jnp.concatenate([qh, zeros_half] if g == 0 else [zeros_half, qh],
                                            axis=0))
            qt4 = jnp.concatenate(cols, axis=1)
            s = jnp.dot(kg, qt4, preferred_element_type=F32)
            s = jnp.where(valid, s, NEG_INF)
            sink = sink_ref[:, g * grp * w:(g + 1) * grp * w]
            m = jnp.maximum(jnp.max(s, axis=0, keepdims=True), sink)
            e = jnp.exp(s - m)
            denom = jnp.sum(e, axis=0, keepdims=True) + jnp.exp(sink - m)
            ot = jnp.dot(vt2[g * HEAD_DIM:(g + 1) * HEAD_DIM, :], e.astype(BF16),
                         preferred_element_type=F32) / denom
            heads += [ot[:, hh * w:(hh + 1) * w] for hh in range(grp)]
        o_ref[nb * w:(nb + 1) * w, :] = jnp.concatenate(heads, axis=0).T.astype(BF16)


def _swa(qt, k, vt, sink_row, batch, seq):
    t = k.shape[0]
    slabs = seq // ROW_TILE
    per = ROW_TILE // SWA_WINDOW

    def prev_idx(b, s):
        return jnp.maximum((b * slabs + s) * per - 1, b * slabs * per)

    return pl.pallas_call(
        _swa_body,
        grid=(batch, slabs),
        in_specs=[
            pl.BlockSpec((per, SWA_WIDTH, SWA_WINDOW), lambda b, s: (b * slabs + s, 0, 0)),
            pl.BlockSpec((ROW_TILE, SWA_KV_WIDTH), lambda b, s: (b * slabs + s, 0)),
            pl.BlockSpec((SWA_WINDOW, SWA_KV_WIDTH), lambda b, s: (prev_idx(b, s), 0)),
            pl.BlockSpec((per, SWA_KV_WIDTH, SWA_WINDOW), lambda b, s: (b * slabs + s, 0, 0)),
            pl.BlockSpec((1, SWA_KV_WIDTH, SWA_WINDOW), lambda b, s: (prev_idx(b, s), 0, 0)),
            pl.BlockSpec((1, SWA_HEADS * SWA_WINDOW), lambda b, s: (0, 0)),
        ],
        out_specs=pl.BlockSpec((ROW_TILE, SWA_WIDTH), lambda b, s: (b * slabs + s, 0)),
        out_shape=jax.ShapeDtypeStruct((t, SWA_WIDTH), BF16),
        compiler_params=pltpu.CompilerParams(
            dimension_semantics=("arbitrary", "arbitrary"), vmem_limit_bytes=VMEM_LIMIT),
        name="swa",
    )(qt, k, k, vt, vt, sink_row)


def _outffn_body(x_ref, mo_ref, so_ref, gmo_ref, gso_ref, wout_ref, gap_ref,
                 gfp_ref, wg_ref, wu_ref, wd_ref, gfo_ref, o_ref):
    mo = _rms(mo_ref[...].astype(F32), gmo_ref[...]).astype(BF16)
    so = _rms(so_ref[...].astype(F32), gso_ref[...]).astype(BF16)
    y = (jnp.dot(mo, wout_ref[0:MOBA_WIDTH, :], preferred_element_type=F32)
         + jnp.dot(so, wout_ref[MOBA_WIDTH:MIX_WIDTH, :], preferred_element_type=F32))
    x1 = x_ref[...] + _rms(y, gap_ref[...])
    h = _rms(x1, gfp_ref[...]).astype(BF16)
    gate = jnp.dot(h, wg_ref[...], preferred_element_type=F32)
    up = jnp.dot(h, wu_ref[...], preferred_element_type=F32)
    f = (gate * jax.nn.sigmoid(gate) * up).astype(BF16)
    d = jnp.dot(f, wd_ref[...], preferred_element_type=F32)
    o_ref[...] = x1 + _rms(d, gfo_ref[...])


def _outffn(x2, mo, so, gmo, gso, wout, gap, gfp, wg, wu, wd, gfo, layer):
    t = x2.shape[0]

    def row(width):
        return pl.BlockSpec((ROW_TILE, width), lambda i: (i, 0))

    def gain(width):
        return pl.BlockSpec((None, 1, width), lambda i: (layer, 0, 0))

    def weight(rows, cols):
        return pl.BlockSpec((None, rows, cols), lambda i: (layer, 0, 0),
                            pipeline_mode=pl.Buffered(1))

    return pl.pallas_call(
        _outffn_body,
        grid=(t // ROW_TILE,),
        in_specs=[
            row(D_MODEL), row(MOBA_WIDTH), row(SWA_WIDTH),
            gain(MOBA_WIDTH), gain(SWA_WIDTH), weight(MIX_WIDTH, D_MODEL), gain(D_MODEL),
            gain(D_MODEL), weight(D_MODEL, D_FF), weight(D_MODEL, D_FF),
            weight(D_FF, D_MODEL), gain(D_MODEL),
        ],
        out_specs=row(D_MODEL),
        out_shape=jax.ShapeDtypeStruct((t, D_MODEL), F32),
        compiler_params=pltpu.CompilerParams(
            dimension_semantics=("arbitrary",), vmem_limit_bytes=VMEM_LIMIT),
        name="outffn",
    )(x2, mo, so, gmo, gso, wout, gap, gfp, wg, wu, wd, gfo)


def _rope_tables(seq):
    half = HEAD_DIM // 2
    inv = 1.0 / (ROPE_THETA ** (jnp.arange(0, HEAD_DIM, 2, dtype=F32) / HEAD_DIM))
    ang = jnp.arange(seq, dtype=F32)[:, None] * inv[None, :]
    cos, sin = jnp.cos(ang), jnp.sin(ang)
    reps = LANES // half
    cos_t = jnp.tile(cos, (1, reps))
    sin_t = jnp.tile(jnp.concatenate([-sin, sin], axis=1), (1, reps // 2))
    scale = HEAD_DIM ** -0.5
    return cos_t * scale, sin_t * scale, cos_t, sin_t


def kernel(x, w_in, b_in, w_out, g_attn_pre, g_attn_post, g_moba_out, g_swa_out, attn_sinks,
           g_ffn_pre, g_ffn_post, w_gate, w_up, w_down):
    batch, seq, _ = x.shape
    assert seq % ROW_TILE == 0 and ROW_TILE % MOBA_BLOCK == 0 and ROW_TILE % SWA_WINDOW == 0
    t = batch * seq
    x2 = x.reshape(t, D_MODEL)
    tabs = _rope_tables(seq)
    w_in_b, w_out_b = w_in.astype(BF16), w_out.astype(BF16)
    w_gate_b, w_up_b, w_down_b = w_gate.astype(BF16), w_up.astype(BF16), w_down.astype(BF16)

    def rows(a):
        return a.reshape(DEPTH, 1, a.shape[-1])

    b_in3 = rows(b_in)
    gap, gpo, gmo, gso = rows(g_attn_pre), rows(g_attn_post), rows(g_moba_out), rows(g_swa_out)
    gfp, gfo = rows(g_ffn_pre), rows(g_ffn_post)
    sink_rows = jnp.repeat(attn_sinks, SWA_WINDOW, axis=1).reshape(DEPTH, 1, SWA_HEADS * SWA_WINDOW)

    for layer in range(DEPTH):
        mqt, mk, mvt, km, sqt, sk, svt = _inproj(x2, gap, w_in_b, b_in3, tabs, layer, seq)
        km = km.reshape(batch, seq // MOBA_BLOCK, MOBA_WIDTH)
        mo = _moba(mqt, mk, mvt, km, batch, seq)
        so = _swa(sqt, sk, svt, sink_rows[layer], batch, seq)
        x2 = _outffn(x2, mo, so, gmo, gso, w_out_b, gpo, gfp, w_gate_b, w_up_b, w_down_b,
                     gfo, layer)
    return x2.reshape(batch, seq, D_MODEL)
```

```python
import functools

import jax
import jax.numpy as jnp
from jax import lax
from jax.experimental import pallas as pl
from jax.experimental.pallas import tpu as pltpu

D_MODEL = 1024
DEPTH = 4
HEAD_DIM = 64
MOBA_HEADS = 8
SWA_HEADS = 8
SWA_KV_HEADS = 2
MOBA_WIDTH = MOBA_HEADS * HEAD_DIM
SWA_WIDTH = SWA_HEADS * HEAD_DIM
SWA_KV_WIDTH = SWA_KV_HEADS * HEAD_DIM
MIX_WIDTH = MOBA_WIDTH + SWA_WIDTH
IN_WIDTH = 3 * MOBA_WIDTH + SWA_WIDTH + 2 * SWA_KV_WIDTH
MOBA_BLOCK = 256
MOBA_TOPK = 3
SWA_WINDOW = 128
D_FF = 2816
ROPE_THETA = 10000.0
EPS = 1e-6
NEG_INF = -1e30

LANES = 128
ROW_TILE = 512
VMEM_LIMIT = 56 * 1024 * 1024

BF16 = jnp.bfloat16
F32 = jnp.float32

_MQ, _MK, _MV = 0, MOBA_WIDTH, 2 * MOBA_WIDTH
_SQ = 3 * MOBA_WIDTH
_SK = _SQ + SWA_WIDTH
_SV = _SK + SWA_KV_WIDTH


def _rms(x, g):
    return x * lax.rsqrt(jnp.mean(x * x, axis=-1, keepdims=True) + EPS) * g


def _inproj_body(x_ref, g_ref, w_ref, b_ref, cq_ref, sq_ref, ck_ref, sk_ref,
                 mqt_o, mk_o, mvt_o, km_o, sqt_o, sk_o, svt_o):
    h = _rms(x_ref[...], g_ref[...]).astype(BF16)
    lane = lax.broadcasted_iota(jnp.int32, (1, LANES), 1)
    first_half = (lane & (HEAD_DIM // 2)) == 0
    cq, sq, ck, sk = cq_ref[...], sq_ref[...], ck_ref[...], sk_ref[...]

    def proj(col):
        return (jnp.dot(h, w_ref[:, col:col + LANES], preferred_element_type=F32)
                + b_ref[:, col:col + LANES])

    def rope(slab, c, s):
        rot = jnp.where(first_half,
                        pltpu.roll(slab, LANES - HEAD_DIM // 2, 1),
                        pltpu.roll(slab, HEAD_DIM // 2, 1))
        return slab * c + rot * s

    n_mb = ROW_TILE // MOBA_BLOCK
    n_sb = ROW_TILE // SWA_WINDOW
    for c in range(MOBA_WIDTH // LANES):
        lo = c * LANES
        q = rope(proj(_MQ + lo), cq, sq)
        k = rope(proj(_MK + lo), ck, sk)
        v = proj(_MV + lo)
        mk_o[:, lo:lo + LANES] = k.astype(BF16)
        for r in range(n_mb):
            rows = slice(r * MOBA_BLOCK, (r + 1) * MOBA_BLOCK)
            mqt_o[lo:lo + LANES, rows] = q[rows].T.astype(BF16)
            mvt_o[lo:lo + LANES, rows] = v[rows].T.astype(BF16)
            km_o[r, :, lo:lo + LANES] = jnp.mean(k[rows], axis=0, keepdims=True)
    for c in range(SWA_WIDTH // LANES):
        lo = c * LANES
        q = rope(proj(_SQ + lo), cq, sq)
        for r in range(n_sb):
            rows = slice(r * SWA_WINDOW, (r + 1) * SWA_WINDOW)
            sqt_o[r, lo:lo + LANES, :] = q[rows].T.astype(BF16)
    sk_o[...] = rope(proj(_SK), ck, sk).astype(BF16)
    v = proj(_SV)
    for r in range(n_sb):
        rows = slice(r * SWA_WINDOW, (r + 1) * SWA_WINDOW)
        svt_o[r, :, :] = v[rows].T.astype(BF16)


def _inproj(x2, g, w, b, tabs, layer, seq):
    t = x2.shape[0]
    n_tiles = t // ROW_TILE
    tiles_per_seq = seq // ROW_TILE
    n_mb = ROW_TILE // MOBA_BLOCK
    n_sb = ROW_TILE // SWA_WINDOW
    tab_spec = pl.BlockSpec((ROW_TILE, LANES), lambda i: (i % tiles_per_seq, 0))
    out_shape = (
        jax.ShapeDtypeStruct((MOBA_WIDTH, t), BF16),
        jax.ShapeDtypeStruct((t, MOBA_WIDTH), BF16),
        jax.ShapeDtypeStruct((MOBA_WIDTH, t), BF16),
        jax.ShapeDtypeStruct((t // MOBA_BLOCK, 1, MOBA_WIDTH), F32),
        jax.ShapeDtypeStruct((t // SWA_WINDOW, SWA_WIDTH, SWA_WINDOW), BF16),
        jax.ShapeDtypeStruct((t, SWA_KV_WIDTH), BF16),
        jax.ShapeDtypeStruct((t // SWA_WINDOW, SWA_KV_WIDTH, SWA_WINDOW), BF16),
    )
    out_specs = (
        pl.BlockSpec((MOBA_WIDTH, ROW_TILE), lambda i: (0, i)),
        pl.BlockSpec((ROW_TILE, MOBA_WIDTH), lambda i: (i, 0)),
        pl.BlockSpec((MOBA_WIDTH, ROW_TILE), lambda i: (0, i)),
        pl.BlockSpec((n_mb, 1, MOBA_WIDTH), lambda i: (i, 0, 0)),
        pl.BlockSpec((n_sb, SWA_WIDTH, SWA_WINDOW), lambda i: (i, 0, 0)),
        pl.BlockSpec((ROW_TILE, SWA_KV_WIDTH), lambda i: (i, 0)),
        pl.BlockSpec((n_sb, SWA_KV_WIDTH, SWA_WINDOW), lambda i: (i, 0, 0)),
    )
    return pl.pallas_call(
        _inproj_body,
        grid=(n_tiles,),
        in_specs=[
            pl.BlockSpec((ROW_TILE, D_MODEL), lambda i: (i, 0)),
            pl.BlockSpec((None, 1, D_MODEL), lambda i: (layer, 0, 0)),
            pl.BlockSpec((None, D_MODEL, IN_WIDTH), lambda i: (layer, 0, 0)),
            pl.BlockSpec((None, 1, IN_WIDTH), lambda i: (layer, 0, 0)),
            tab_spec, tab_spec, tab_spec, tab_spec,
        ],
        out_specs=out_specs,
        out_shape=out_shape,
        compiler_params=pltpu.CompilerParams(
            dimension_semantics=("arbitrary",), vmem_limit_bytes=VMEM_LIMIT),
        name="inproj",
    )(x2, g, w, b, *tabs)


def _moba_body(qt_ref, k_ref, vt_ref, km_ref, o_ref):
    n_blocks = km_ref.shape[0]
    seq = qt_ref.shape[1]
    blk = MOBA_BLOCK
    drow = lax.broadcasted_iota(jnp.int32, (LANES, 1), 0)
    lane = lax.broadcasted_iota(jnp.int32, (1, LANES), 1)

    km = km_ref[...]
    km2 = jnp.concatenate([jnp.where(lane < HEAD_DIM, km, 0.0),
                           jnp.where(lane >= HEAD_DIM, km, 0.0)], axis=0)
    hi = km2.astype(BF16)
    r1 = km2 - hi.astype(F32)
    mid = r1.astype(BF16)
    low = (r1 - mid.astype(F32)).astype(BF16)
    g3 = jnp.dot(jnp.concatenate([hi, mid, low], axis=0), qt_ref[...],
                 preferred_element_type=F32)
    nb2 = 2 * n_blocks
    gate = g3[0:nb2] + g3[nb2:2 * nb2] + g3[2 * nb2:3 * nb2]

    n_idx = lax.broadcasted_iota(jnp.int32, (n_blocks, seq), 0)
    q_blk = lax.broadcasted_iota(jnp.int32, (n_blocks, seq), 1) // blk
    past = n_idx < q_blk
    bias = []
    for hh in range(2):
        gm = jnp.where(past, gate[hh * n_blocks:(hh + 1) * n_blocks], -jnp.inf)
        rank = jnp.zeros((n_blocks, seq), jnp.int32)
        for r in range(1, n_blocks):
            gr = pltpu.roll(gm, r, 0)
            nr = (n_idx - r) & (n_blocks - 1)
            beats = (gr > gm) | ((gr == gm) & (nr < n_idx))
            rank = rank + beats.astype(jnp.int32)
        sel = past & (rank < MOBA_TOPK)
        bias.append(jnp.where(sel, 0.0, NEG_INF))

    kpos = lax.broadcasted_iota(jnp.int32, (blk, blk), 0)
    qpos = lax.broadcasted_iota(jnp.int32, (blk, blk), 1)
    causal = kpos <= qpos

    for i in range(n_blocks):
        cols = slice(i * blk, (i + 1) * blk)
        n_keys = (i + 1) * blk
        qt = qt_ref[:, cols]
        outs = []
        for hh in range(2):
            qth = jnp.where((drow >= hh * HEAD_DIM) & (drow < (hh + 1) * HEAD_DIM), qt,
                            jnp.zeros_like(qt))
            s = jnp.dot(k_ref[0:n_keys, :], qth, preferred_element_type=F32)
            parts = [s[j * blk:(j + 1) * blk] + bias[hh][j:j + 1, cols] for j in range(i)]
            parts.append(jnp.where(causal, s[i * blk:n_keys], NEG_INF))
            m = parts[0].max(axis=0, keepdims=True)
            for part in parts[1:]:
                m = jnp.maximum(m, part.max(axis=0, keepdims=True))
            p = [jnp.exp(part - m) for part in parts]
            denom = p[0].sum(axis=0, keepdims=True)
            for pj in p[1:]:
                denom = denom + pj.sum(axis=0, keepdims=True)
            pt = jnp.concatenate([pj.astype(BF16) for pj in p], axis=0)
            acc = jnp.dot(vt_ref[hh * HEAD_DIM:(hh + 1) * HEAD_DIM, 0:n_keys], pt,
                          preferred_element_type=F32)
            outs.append(acc / denom)
        o_ref[cols, :] = jnp.concatenate(outs, axis=0).T.astype(BF16)


def _moba(qt, k, vt, km, batch, seq):
    t = k.shape[0]
    n_blocks = seq // MOBA_BLOCK
    n_pairs = MOBA_WIDTH // LANES
    return pl.pallas_call(
        _moba_body,
        grid=(batch, n_pairs),
        in_specs=[
            pl.BlockSpec((LANES, seq), lambda b, p: (p, b)),
            pl.BlockSpec((seq, LANES), lambda b, p: (b, p)),
            pl.BlockSpec((LANES, seq), lambda b, p: (p, b)),
            pl.BlockSpec((None, n_blocks, LANES), lambda b, p: (b, 0, p)),
        ],
        out_specs=pl.BlockSpec((seq, LANES), lambda b, p: (b, p)),
        out_shape=jax.ShapeDtypeStruct((t, MOBA_WIDTH), BF16),
        compiler_params=pltpu.CompilerParams(
            dimension_semantics=("arbitrary", "arbitrary"), vmem_limit_bytes=VMEM_LIMIT),
        name="moba",
    )(qt, k, vt, km)


def _swa_body(qt_ref, kc_ref, kp_ref, vtc_ref, vtp_ref, sink_ref, o_ref):
    slab = pl.program_id(1)
    w = SWA_WINDOW
    grp = SWA_HEADS // SWA_KV_HEADS
    lane = lax.broadcasted_iota(jnp.int32, (1, LANES), 1)
    kpos = lax.broadcasted_iota(jnp.int32, (2 * w, grp * w), 0)
    qpos = lax.broadcasted_iota(jnp.int32, (2 * w, grp * w), 1) & (w - 1)
    diff = kpos - qpos
    in_window = (diff >= 1) & (diff <= w)
    zeros_half = jnp.zeros((HEAD_DIM, w), BF16)

    for nb in range(ROW_TILE // w):
        if nb == 0:
            k_prev, vt_prev = kp_ref[...], vtp_ref[0]
            valid = in_window & ((kpos >= w) | (slab > 0))
        else:
            k_prev, vt_prev = kc_ref[(nb - 1) * w:nb * w, :], vtc_ref[nb - 1]
            valid = in_window
        k2 = jnp.concatenate([k_prev, kc_ref[nb * w:(nb + 1) * w, :]], axis=0)
        vt2 = jnp.concatenate([vt_prev, vtc_ref[nb]], axis=1)
        heads = []
        for g in range(SWA_KV_HEADS):
            kg = jnp.where((lane >= g * HEAD_DIM) & (lane < (g + 1) * HEAD_DIM), k2,
                           jnp.zeros_like(k2))
            cols = []
            for hh in range(grp):
                h = g * grp + hh
                qh = qt_ref[nb, h * HEAD_DIM:(h + 1) * HEAD_DIM, :]
                cols.append(jnp.concatenate([qh, zeros_half] if g == 0 else [zeros_half, qh],
                                            axis=0))
            qt4 = jnp.concatenate(cols, axis=1)
            s = jnp.dot(kg, qt4, preferred_element_type=F32)
            s = jnp.where(valid, s, NEG_INF)
            sink = sink_ref[:, g * grp * w:(g + 1) * grp * w]
            m = jnp.maximum(jnp.max(s, axis=0, keepdims=True), sink)
            e = jnp.exp(s - m)
            denom = jnp.sum(e, axis=0, keepdims=True) + jnp.exp(sink - m)
            ot = jnp.dot(vt2[g * HEAD_DIM:(g + 1) * HEAD_DIM, :], e.astype(BF16),
                         preferred_element_type=F32) / denom
            heads += [ot[:, hh * w:(hh + 1) * w] for hh in range(grp)]
        o_ref[nb * w:(nb + 1) * w, :] = jnp.concatenate(heads, axis=0).T.astype(BF16)


def _swa(qt, k, vt, sink_row, batch, seq):
    t = k.shape[0]
    slabs = seq // ROW_TILE
    per = ROW_TILE // SWA_WINDOW

    def prev_idx(b, s):
        return jnp.maximum((b * slabs + s) * per - 1, b * slabs * per)

    return pl.pallas_call(
        _swa_body,
        grid=(batch, slabs),
        in_specs=[
            pl.BlockSpec((per, SWA_WIDTH, SWA_WINDOW), lambda b, s: (b * slabs + s, 0, 0)),
            pl.BlockSpec((ROW_TILE, SWA_KV_WIDTH), lambda b, s: (b * slabs + s, 0)),
            pl.BlockSpec((SWA_WINDOW, SWA_KV_WIDTH), lambda b, s: (prev_idx(b, s), 0)),
            pl.BlockSpec((per, SWA_KV_WIDTH, SWA_WINDOW), lambda b, s: (b * slabs + s, 0, 0)),
            pl.BlockSpec((1, SWA_KV_WIDTH, SWA_WINDOW), lambda b, s: (prev_idx(b, s), 0, 0)),
            pl.BlockSpec((1, SWA_HEADS * SWA_WINDOW), lambda b, s: (0, 0)),
        ],
        out_specs=pl.BlockSpec((ROW_TILE, SWA_WIDTH), lambda b, s: (b * slabs + s, 0)),
        out_shape=jax.ShapeDtypeStruct((t, SWA_WIDTH), BF16),
        compiler_params=pltpu.CompilerParams(
            dimension_semantics=("arbitrary", "arbitrary"), vmem_limit_bytes=VMEM_LIMIT),
        name="swa",
    )(qt, k, k, vt, vt, sink_row)


def _outffn_body(x_ref, mo_ref, so_ref, gmo_ref, gso_ref, wout_ref, gap_ref,
                 gfp_ref, wg_ref, wu_ref, wd_ref, gfo_ref, o_ref):
    mo = _rms(mo_ref[...].astype(F32), gmo_ref[...]).astype(BF16)
    so = _rms(so_ref[...].astype(F32), gso_ref[...]).astype(BF16)
    y = (jnp.dot(mo, wout_ref[0:MOBA_WIDTH, :], preferred_element_type=F32)
         + jnp.dot(so, wout_ref[MOBA_WIDTH:MIX_WIDTH, :], preferred_element_type=F32))
    x1 = x_ref[...] + _rms(y, gap_ref[...])
    h = _rms(x1, gfp_ref[...]).astype(BF16)
    gate = jnp.dot(h, wg_ref[...], preferred_element_type=F32)
    up = jnp.dot(h, wu_ref[...], preferred_element_type=F32)
    f = (gate * jax.nn.sigmoid(gate) * up).astype(BF16)
    d = jnp.dot(f, wd_ref[...], preferred_element_type=F32)
    o_ref[...] = x1 + _rms(d, gfo_ref[...])


def _outffn(x2, mo, so, gmo, gso, wout, gap, gfp, wg, wu, wd, gfo, layer):
    t = x2.shape[0]

    def row(width):
        return pl.BlockSpec((ROW_TILE, width), lambda i: (i, 0))

    def gain(width):
        return pl.BlockSpec((None, 1, width), lambda i: (layer, 0, 0))

    def weight(rows, cols):
        return pl.BlockSpec((None, rows, cols), lambda i: (layer, 0, 0),
                            pipeline_mode=pl.Buffered(1))

    return pl.pallas_call(
        _outffn_body,
        grid=(t // ROW_TILE,),
        in_specs=[
            row(D_MODEL), row(MOBA_WIDTH), row(SWA_WIDTH),
            gain(MOBA_WIDTH), gain(SWA_WIDTH), weight(MIX_WIDTH, D_MODEL), gain(D_MODEL),
            gain(D_MODEL), weight(D_MODEL, D_FF), weight(D_MODEL, D_FF),
            weight(D_FF, D_MODEL), gain(D_MODEL),
        ],
        out_specs=row(D_MODEL),
        out_shape=jax.ShapeDtypeStruct((t, D_MODEL), F32),
        compiler_params=pltpu.CompilerParams(
            dimension_semantics=("arbitrary",), vmem_limit_bytes=VMEM_LIMIT),
        name="outffn",
    )(x2, mo, so, gmo, gso, wout, gap, gfp, wg, wu, wd, gfo)


def _rope_tables(seq):
    half = HEAD_DIM // 2
    inv = 1.0 / (ROPE_THETA ** (jnp.arange(0, HEAD_DIM, 2, dtype=F32) / HEAD_DIM))
    ang = jnp.arange(seq, dtype=F32)[:, None] * inv[None, :]
    cos, sin = jnp.cos(ang), jnp.sin(ang)
    reps = LANES // half
    cos_t = jnp.tile(cos, (1, reps))
    sin_t = jnp.tile(jnp.concatenate([-sin, sin], axis=1), (1, reps // 2))
    scale = HEAD_DIM ** -0.5
    return cos_t * scale, sin_t * scale, cos_t, sin_t


def kernel(x, w_in, b_in, w_out, g_attn_pre, g_attn_post, g_moba_out, g_swa_out, attn_sinks,
           g_ffn_pre, g_ffn_post, w_gate, w_up, w_down):
    batch, seq, _ = x.shape
    assert seq % ROW_TILE == 0 and ROW_TILE % MOBA_BLOCK == 0 and ROW_TILE % SWA_WINDOW == 0
    t = batch * seq
    x2 = x.reshape(t, D_MODEL)
    tabs = _rope_tables(seq)
    w_in_b, w_out_b = w_in.astype(BF16), w_out.astype(BF16)
    w_gate_b, w_up_b, w_down_b = w_gate.astype(BF16), w_up.astype(BF16), w_down.astype(BF16)

    def rows(a):
        return a.reshape(DEPTH, 1, a.shape[-1])

    b_in3 = rows(b_in)
    gap, gpo, gmo, gso = rows(g_attn_pre), rows(g_attn_post), rows(g_moba_out), rows(g_swa_out)
    gfp, gfo = rows(g_ffn_pre), rows(g_ffn_post)
    sink_rows = jnp.repeat(attn_sinks, SWA_WINDOW, axis=1).reshape(DEPTH, 1, SWA_HEADS * SWA_WINDOW)

    for layer in range(DEPTH):
        mqt, mk, mvt, km, sqt, sk, svt = _inproj(x2, gap, w_in_b, b_in3, tabs, layer, seq)
        km = km.reshape(batch, seq // MOBA_BLOCK, MOBA_WIDTH)
        mo = _moba(mqt, mk, mvt, km, batch, seq)
        so = _swa(sqt, sk, svt, sink_rows[layer], batch, seq)
        x2 = _outffn(x2, mo, so, gmo, gso, w_out_b, gpo, gfp, w_gate_b, w_up_b, w_down_b,
                     gfo, layer)
    return x2.reshape(batch, seq, D_MODEL)
```

```python
import functools

import jax
import jax.numpy as jnp
from jax import lax
from jax.experimental import pallas as pl
from jax.experimental.pallas import tpu as pltpu

D_MODEL = 1024
DEPTH = 4
HEAD_DIM = 64
MOBA_HEADS = 8
SWA_HEADS = 8
SWA_KV_HEADS = 2
MOBA_WIDTH = MOBA_HEADS * HEAD_DIM
SWA_WIDTH = SWA_HEADS * HEAD_DIM
SWA_KV_WIDTH = SWA_KV_HEADS * HEAD_DIM
MIX_WIDTH = MOBA_WIDTH + SWA_WIDTH
IN_WIDTH = 3 * MOBA_WIDTH + SWA_WIDTH + 2 * SWA_KV_WIDTH
MOBA_BLOCK = 256
MOBA_TOPK = 3
SWA_WINDOW = 128
D_FF = 2816
ROPE_THETA = 10000.0
EPS = 1e-6
NEG_INF = -1e30
LOG2E = 1.4426950408889634

LANES = 128
ROW_TILE = 512
VMEM_LIMIT = 56 * 1024 * 1024

BF16 = jnp.bfloat16
F32 = jnp.float32

_MQ, _MK, _MV = 0, MOBA_WIDTH, 2 * MOBA_WIDTH
_SQ = 3 * MOBA_WIDTH
_SK = _SQ + SWA_WIDTH
_SV = _SK + SWA_KV_WIDTH


def _rms(x, g):
    return x * lax.rsqrt(jnp.mean(x * x, axis=-1, keepdims=True) + EPS) * g


def _inproj_body(x_ref, g_ref, w_ref, b_ref, cq_ref, sq_ref, ck_ref, sk_ref,
                 mqt_o, mk_o, mvt_o, km_o, sqt_o, sk_o, svt_o):
    h = _rms(x_ref[...], g_ref[...]).astype(BF16)
    lane = lax.broadcasted_iota(jnp.int32, (1, LANES), 1)
    first_half = (lane & (HEAD_DIM // 2)) == 0
    cq, sq, ck, sk = cq_ref[...], sq_ref[...], ck_ref[...], sk_ref[...]

    def section(col, width):
        return (jnp.dot(h, w_ref[:, col:col + width], preferred_element_type=F32)
                + b_ref[:, col:col + width])

    def rope(slab, c, s):
        rot = jnp.where(first_half,
                        pltpu.roll(slab, LANES - HEAD_DIM // 2, 1),
                        pltpu.roll(slab, HEAD_DIM // 2, 1))
        return slab * c + rot * s

    n_mb = ROW_TILE // MOBA_BLOCK
    n_sb = ROW_TILE // SWA_WINDOW
    mq_all = section(_MQ, MOBA_WIDTH)
    mk_all = section(_MK, MOBA_WIDTH)
    mv_all = section(_MV, MOBA_WIDTH)
    for c in range(MOBA_WIDTH // LANES):
        lo = c * LANES
        q = rope(mq_all[:, lo:lo + LANES], cq, sq)
        k = rope(mk_all[:, lo:lo + LANES], ck, sk)
        v = mv_all[:, lo:lo + LANES]
        mk_o[:, lo:lo + LANES] = k.astype(BF16)
        for r in range(n_mb):
            rows = slice(r * MOBA_BLOCK, (r + 1) * MOBA_BLOCK)
            mqt_o[lo:lo + LANES, rows] = q[rows].T.astype(BF16)
            mvt_o[lo:lo + LANES, rows] = v[rows].T.astype(BF16)
            km_o[r, :, lo:lo + LANES] = jnp.mean(k[rows], axis=0, keepdims=True)
    sq_all = section(_SQ, SWA_WIDTH)
    for c in range(SWA_WIDTH // LANES):
        lo = c * LANES
        q = rope(sq_all[:, lo:lo + LANES], cq, sq)
        for r in range(n_sb):
            rows = slice(r * SWA_WINDOW, (r + 1) * SWA_WINDOW)
            sqt_o[r, lo:lo + LANES, :] = q[rows].T.astype(BF16)
    skv = section(_SK, 2 * SWA_KV_WIDTH)
    sk_o[...] = rope(skv[:, 0:SWA_KV_WIDTH], ck, sk).astype(BF16)
    v = skv[:, SWA_KV_WIDTH:2 * SWA_KV_WIDTH]
    for r in range(n_sb):
        rows = slice(r * SWA_WINDOW, (r + 1) * SWA_WINDOW)
        svt_o[r, :, :] = v[rows].T.astype(BF16)


def _inproj(x2, g, w, b, tabs, layer, seq):
    t = x2.shape[0]
    n_tiles = t // ROW_TILE
    tiles_per_seq = seq // ROW_TILE
    n_mb = ROW_TILE // MOBA_BLOCK
    n_sb = ROW_TILE // SWA_WINDOW
    tab_spec = pl.BlockSpec((ROW_TILE, LANES), lambda i: (i % tiles_per_seq, 0))
    out_shape = (
        jax.ShapeDtypeStruct((MOBA_WIDTH, t), BF16),
        jax.ShapeDtypeStruct((t, MOBA_WIDTH), BF16),
        jax.ShapeDtypeStruct((MOBA_WIDTH, t), BF16),
        jax.ShapeDtypeStruct((t // MOBA_BLOCK, 1, MOBA_WIDTH), F32),
        jax.ShapeDtypeStruct((t // SWA_WINDOW, SWA_WIDTH, SWA_WINDOW), BF16),
        jax.ShapeDtypeStruct((t, SWA_KV_WIDTH), BF16),
        jax.ShapeDtypeStruct((t // SWA_WINDOW, SWA_KV_WIDTH, SWA_WINDOW), BF16),
    )
    out_specs = (
        pl.BlockSpec((MOBA_WIDTH, ROW_TILE), lambda i: (0, i)),
        pl.BlockSpec((ROW_TILE, MOBA_WIDTH), lambda i: (i, 0)),
        pl.BlockSpec((MOBA_WIDTH, ROW_TILE), lambda i: (0, i)),
        pl.BlockSpec((n_mb, 1, MOBA_WIDTH), lambda i: (i, 0, 0)),
        pl.BlockSpec((n_sb, SWA_WIDTH, SWA_WINDOW), lambda i: (i, 0, 0)),
        pl.BlockSpec((ROW_TILE, SWA_KV_WIDTH), lambda i: (i, 0)),
        pl.BlockSpec((n_sb, SWA_KV_WIDTH, SWA_WINDOW), lambda i: (i, 0, 0)),
    )
    return pl.pallas_call(
        _inproj_body,
        grid=(n_tiles,),
        in_specs=[
            pl.BlockSpec((ROW_TILE, D_MODEL), lambda i: (i, 0)),
            pl.BlockSpec((None, 1, D_MODEL), lambda i: (layer, 0, 0)),
            pl.BlockSpec((None, D_MODEL, IN_WIDTH), lambda i: (layer, 0, 0)),
            pl.BlockSpec((None, 1, IN_WIDTH), lambda i: (layer, 0, 0)),
            tab_spec, tab_spec, tab_spec, tab_spec,
        ],
        out_specs=out_specs,
        out_shape=out_shape,
        compiler_params=pltpu.CompilerParams(
            dimension_semantics=("arbitrary",), vmem_limit_bytes=VMEM_LIMIT),
        name="inproj",
    )(x2, g, w, b, *tabs)


def _moba_body(qt_ref, k_ref, vt_ref, km_ref, o_ref):
    n_blocks = km_ref.shape[0]
    seq = qt_ref.shape[1]
    blk = MOBA_BLOCK
    drow = lax.broadcasted_iota(jnp.int32, (LANES, 1), 0)
    lane = lax.broadcasted_iota(jnp.int32, (1, LANES), 1)

    km = km_ref[...]
    km2 = jnp.concatenate([jnp.where(lane < HEAD_DIM, km, 0.0),
                           jnp.where(lane >= HEAD_DIM, km, 0.0)], axis=0)
    hi = km2.astype(BF16)
    r1 = km2 - hi.astype(F32)
    mid = r1.astype(BF16)
    low = (r1 - mid.astype(F32)).astype(BF16)
    g3 = jnp.dot(jnp.concatenate([hi, mid, low], axis=0), qt_ref[...],
                 preferred_element_type=F32)
    nb2 = 2 * n_blocks
    gate = g3[0:nb2] + g3[nb2:2 * nb2] + g3[2 * nb2:3 * nb2]

    n_idx = lax.broadcasted_iota(jnp.int32, (n_blocks, seq), 0)
    q_blk = lax.broadcasted_iota(jnp.int32, (n_blocks, seq), 1) // blk
    past = n_idx < q_blk
    bias = []
    for hh in range(2):
        gm = jnp.where(past, gate[hh * n_blocks:(hh + 1) * n_blocks], -jnp.inf)
        rank = jnp.zeros((n_blocks, seq), jnp.int32)
        for r in range(1, n_blocks):
            gr = pltpu.roll(gm, r, 0)
            nr = (n_idx - r) & (n_blocks - 1)
            beats = (gr > gm) | ((gr == gm) & (nr < n_idx))
            rank = rank + beats.astype(jnp.int32)
        sel = past & (rank < MOBA_TOPK)
        bias.append(jnp.where(sel, 0.0, NEG_INF))

    kpos = lax.broadcasted_iota(jnp.int32, (blk, blk), 0)
    qpos = lax.broadcasted_iota(jnp.int32, (blk, blk), 1)
    causal = kpos <= qpos

    def scores(i):
        qt = qt_ref[:, i * blk:(i + 1) * blk]
        qt2 = jnp.concatenate(
            [jnp.where((drow >= hh * HEAD_DIM) & (drow < (hh + 1) * HEAD_DIM), qt,
                       jnp.zeros_like(qt)) for hh in range(2)], axis=1)
        return jnp.dot(k_ref[0:(i + 1) * blk, :], qt2, preferred_element_type=F32)

    causal2 = jnp.concatenate([causal, causal], axis=1)
    s_next = scores(0)
    for i in range(n_blocks):
        cols = slice(i * blk, (i + 1) * blk)
        n_keys = (i + 1) * blk
        s = s_next
        if i + 1 < n_blocks:
            s_next = scores(i + 1)
        if i > MOBA_TOPK:
            b2 = jnp.concatenate([bias[0][:, cols], bias[1][:, cols]], axis=1)
            parts = [s[j * blk:(j + 1) * blk] + b2[j:j + 1] for j in range(i)]
        else:
            parts = [s[j * blk:(j + 1) * blk] for j in range(i)]
        parts.append(jnp.where(causal2, s[i * blk:n_keys], NEG_INF))
        m = parts[0].max(axis=0, keepdims=True)
        for part in parts[1:]:
            m = jnp.maximum(m, part.max(axis=0, keepdims=True))
        p = [jnp.exp2(part - m) for part in parts]
        denom = p[0].sum(axis=0, keepdims=True)
        for pj in p[1:]:
            denom = denom + pj.sum(axis=0, keepdims=True)
        pt = jnp.concatenate([pj.astype(BF16) for pj in p], axis=0)
        acc = jnp.dot(vt_ref[:, 0:n_keys], pt, preferred_element_type=F32)
        ot = jnp.concatenate([acc[0:HEAD_DIM, 0:blk] / denom[:, 0:blk],
                              acc[HEAD_DIM:LANES, blk:2 * blk] / denom[:, blk:2 * blk]], axis=0)
        o_ref[cols, :] = ot.T.astype(BF16)


def _moba(qt, k, vt, km, batch, seq):
    t = k.shape[0]
    n_blocks = seq // MOBA_BLOCK
    n_pairs = MOBA_WIDTH // LANES
    return pl.pallas_call(
        _moba_body,
        grid=(batch, n_pairs),
        in_specs=[
            pl.BlockSpec((LANES, seq), lambda b, p: (p, b)),
            pl.BlockSpec((seq, LANES), lambda b, p: (b, p)),
            pl.BlockSpec((LANES, seq), lambda b, p: (p, b)),
            pl.BlockSpec((None, n_blocks, LANES), lambda b, p: (b, 0, p)),
        ],
        out_specs=pl.BlockSpec((seq, LANES), lambda b, p: (b, p)),
        out_shape=jax.ShapeDtypeStruct((t, MOBA_WIDTH), BF16),
        compiler_params=pltpu.CompilerParams(
            dimension_semantics=("arbitrary", "arbitrary"), vmem_limit_bytes=VMEM_LIMIT),
        name="moba",
    )(qt, k, vt, km)


def _swa_body(qt_ref, kc_ref, kp_ref, vtc_ref, vtp_ref, sink_ref, o_ref):
    slab = pl.program_id(1)
    w = SWA_WINDOW
    grp = SWA_HEADS // SWA_KV_HEADS
    kpos = lax.broadcasted_iota(jnp.int32, (2 * w, w), 0)
    qpos = lax.broadcasted_iota(jnp.int32, (2 * w, w), 1)
    diff = kpos - qpos
    in_window = (diff >= 1) & (diff <= w)
    first_window = in_window & ((kpos >= w) | (slab > 0))
    zeros_half = jnp.zeros((HEAD_DIM, w), BF16)
    sinks = sink_ref[...] * LOG2E

    def scores(nb):
        k_prev = kp_ref[...] if nb == 0 else kc_ref[(nb - 1) * w:nb * w, :]
        k2 = jnp.concatenate([k_prev, kc_ref[nb * w:(nb + 1) * w, :]], axis=0)
        cols = []
        for h in range(SWA_HEADS):
            qh = qt_ref[nb, h * HEAD_DIM:(h + 1) * HEAD_DIM, :]
            cols.append(jnp.concatenate([qh, zeros_half] if h < grp else [zeros_half, qh],
                                        axis=0))
        return jnp.dot(k2, jnp.concatenate(cols, axis=1), preferred_element_type=F32)

    n_sb = ROW_TILE // w
    s_next = scores(0)
    for nb in range(n_sb):
        s = s_next
        if nb + 1 < n_sb:
            s_next = scores(nb + 1)
        valid = first_window if nb == 0 else in_window
        vt_prev = vtp_ref[0] if nb == 0 else vtc_ref[nb - 1]
        vt2 = jnp.concatenate([vt_prev, vtc_ref[nb]], axis=1)
        es, denoms = [], []
        for h in range(SWA_HEADS):
            sh = jnp.where(valid, s[:, h * w:(h + 1) * w], NEG_INF)
            sink = sinks[:, h * w:(h + 1) * w]
            m = jnp.maximum(jnp.max(sh, axis=0, keepdims=True), sink)
            e = jnp.exp2(sh - m)
            denoms.append(jnp.sum(e, axis=0, keepdims=True) + jnp.exp2(sink - m))
            es.append(e.astype(BF16))
        acc = jnp.dot(vt2, jnp.concatenate(es, axis=1), preferred_element_type=F32)
        heads = []
        for h in range(SWA_HEADS):
            g = h // grp
            heads.append(acc[g * HEAD_DIM:(g + 1) * HEAD_DIM, h * w:(h + 1) * w] / denoms[h])
        o_ref[nb * w:(nb + 1) * w, :] = jnp.concatenate(heads, axis=0).T.astype(BF16)


def _swa(qt, k, vt, sink_row, batch, seq):
    t = k.shape[0]
    slabs = seq // ROW_TILE
    per = ROW_TILE // SWA_WINDOW

    def prev_idx(b, s):
        return jnp.maximum((b * slabs + s) * per - 1, b * slabs * per)

    return pl.pallas_call(
        _swa_body,
        grid=(batch, slabs),
        in_specs=[
            pl.BlockSpec((per, SWA_WIDTH, SWA_WINDOW), lambda b, s: (b * slabs + s, 0, 0)),
            pl.BlockSpec((ROW_TILE, SWA_KV_WIDTH), lambda b, s: (b * slabs + s, 0)),
            pl.BlockSpec((SWA_WINDOW, SWA_KV_WIDTH), lambda b, s: (prev_idx(b, s), 0)),
            pl.BlockSpec((per, SWA_KV_WIDTH, SWA_WINDOW), lambda b, s: (b * slabs + s, 0, 0)),
            pl.BlockSpec((1, SWA_KV_WIDTH, SWA_WINDOW), lambda b, s: (prev_idx(b, s), 0, 0)),
            pl.BlockSpec((1, SWA_HEADS * SWA_WINDOW), lambda b, s: (0, 0)),
        ],
        out_specs=pl.BlockSpec((ROW_TILE, SWA_WIDTH), lambda b, s: (b * slabs + s, 0)),
        out_shape=jax.ShapeDtypeStruct((t, SWA_WIDTH), BF16),
        compiler_params=pltpu.CompilerParams(
            dimension_semantics=("arbitrary", "arbitrary"), vmem_limit_bytes=VMEM_LIMIT),
        name="swa",
    )(qt, k, k, vt, vt, sink_row)


def _outffn_body(x_ref, mo_ref, so_ref, gmo_ref, gso_ref, wout_ref, gap_ref,
                 gfp_ref, wg_ref, wu_ref, wd_ref, gfo_ref, o_ref):
    mo = _rms(mo_ref[...].astype(F32), gmo_ref[...]).astype(BF16)
    so = _rms(so_ref[...].astype(F32), gso_ref[...]).astype(BF16)
    y = (jnp.dot(mo, wout_ref[0:MOBA_WIDTH, :], preferred_element_type=F32)
         + jnp.dot(so, wout_ref[MOBA_WIDTH:MIX_WIDTH, :], preferred_element_type=F32))
    x1 = x_ref[...] + _rms(y, gap_ref[...])
    h = _rms(x1, gfp_ref[...]).astype(BF16)
    gate = jnp.dot(h, wg_ref[...], preferred_element_type=F32)
    up = jnp.dot(h, wu_ref[...], preferred_element_type=F32)
    f = (gate * jax.nn.sigmoid(gate) * up).astype(BF16)
    d = jnp.dot(f, wd_ref[...], preferred_element_type=F32)
    o_ref[...] = x1 + _rms(d, gfo_ref[...])


def _outffn(x2, mo, so, gmo, gso, wout, gap, gfp, wg, wu, wd, gfo, layer):
    t = x2.shape[0]

    def row(width):
        return pl.BlockSpec((ROW_TILE, width), lambda i: (i, 0))

    def gain(width):
        return pl.BlockSpec((None, 1, width), lambda i: (layer, 0, 0))

    def weight(rows, cols):
        return pl.BlockSpec((None, rows, cols), lambda i: (layer, 0, 0),
                            pipeline_mode=pl.Buffered(1))

    return pl.pallas_call(
        _outffn_body,
        grid=(t // ROW_TILE,),
        in_specs=[
            row(D_MODEL), row(MOBA_WIDTH), row(SWA_WIDTH),
            gain(MOBA_WIDTH), gain(SWA_WIDTH), weight(MIX_WIDTH, D_MODEL), gain(D_MODEL),
            gain(D_MODEL), weight(D_MODEL, D_FF), weight(D_MODEL, D_FF),
            weight(D_FF, D_MODEL), gain(D_MODEL),
        ],
        out_specs=row(D_MODEL),
        out_shape=jax.ShapeDtypeStruct((t, D_MODEL), F32),
        compiler_params=pltpu.CompilerParams(
            dimension_semantics=("arbitrary",), vmem_limit_bytes=VMEM_LIMIT),
        name="outffn",
    )(x2, mo, so, gmo, gso, wout, gap, gfp, wg, wu, wd, gfo)


def _rope_tables(seq):
    half = HEAD_DIM // 2
    inv = 1.0 / (ROPE_THETA ** (jnp.arange(0, HEAD_DIM, 2, dtype=F32) / HEAD_DIM))
    ang = jnp.arange(seq, dtype=F32)[:, None] * inv[None, :]
    cos, sin = jnp.cos(ang), jnp.sin(ang)
    reps = LANES // half
    cos_t = jnp.tile(cos, (1, reps))
    sin_t = jnp.tile(jnp.concatenate([-sin, sin], axis=1), (1, reps // 2))
    scale = HEAD_DIM ** -0.5 * LOG2E
    return cos_t * scale, sin_t * scale, cos_t, sin_t


def kernel(x, w_in, b_in, w_out, g_attn_pre, g_attn_post, g_moba_out, g_swa_out, attn_sinks,
           g_ffn_pre, g_ffn_post, w_gate, w_up, w_down):
    batch, seq, _ = x.shape
    assert seq % ROW_TILE == 0 and ROW_TILE % MOBA_BLOCK == 0 and ROW_TILE % SWA_WINDOW == 0
    t = batch * seq
    x2 = x.reshape(t, D_MODEL)
    tabs = _rope_tables(seq)
    w_in_b, w_out_b = w_in.astype(BF16), w_out.astype(BF16)
    w_gate_b, w_up_b, w_down_b = w_gate.astype(BF16), w_up.astype(BF16), w_down.astype(BF16)

    def rows(a):
        return a.reshape(DEPTH, 1, a.shape[-1])

    b_in3 = rows(b_in)
    gap, gpo, gmo, gso = rows(g_attn_pre), rows(g_attn_post), rows(g_moba_out), rows(g_swa_out)
    gfp, gfo = rows(g_ffn_pre), rows(g_ffn_post)
    sink_rows = jnp.repeat(attn_sinks, SWA_WINDOW, axis=1).reshape(DEPTH, 1, SWA_HEADS * SWA_WINDOW)

    for layer in range(DEPTH):
        mqt, mk, mvt, km, sqt, sk, svt = _inproj(x2, gap, w_in_b, b_in3, tabs, layer, seq)
        km = km.reshape(batch, seq // MOBA_BLOCK, MOBA_WIDTH)
        mo = _moba(mqt, mk, mvt, km, batch, seq)
        so = _swa(sqt, sk, svt, sink_rows[layer], batch, seq)
        x2 = _outffn(x2, mo, so, gmo, gso, w_out_b, gpo, gfp, w_gate_b, w_up_b, w_down_b,
                     gfo, layer)
    return x2.reshape(batch, seq, D_MODEL)
```
